```python
import math
import numpy as np
import jax
import jax.numpy as jnp
from jax import lax

D_MODEL = 1024
BATCH = 4
SEQ = 8192
DEPTH = 4

GRID_W = 64
CTX_LEN = 256

H_A = 4
DK_A = 128
DV_A = 128
CONV_K = 5
GDN_CHUNK = 64
HQ_B = 8
HKV_B = 2
GROUP_B = HQ_B // HKV_B
DH_B = 64
WINDOW = 128
WIN_BLOCK = 128
ROPE_BASE = 10000.0
H_C = 4
DK_C = 128
DV_C = 128
HGRN_CHUNK = 64
N_GROUPS = 4
EXPERTS_PER_GROUP = 8
N_EXPERTS = N_GROUPS * EXPERTS_PER_GROUP
TOP_K_IN_GROUP = 2
D_EXPERT = 512
MOE_BLOCK = 128

N_BRANCH = 3
DEEPNORM_ALPHA = (2 * DEPTH) ** 0.25
DEEPNORM_BETA = (8 * DEPTH) ** -0.25
LN_EPS = 1e-5
RMS_EPS = 1e-6
L2_EPS = 1e-6
MASK_VALUE = -1e30
F_FLOOR = 1e-30

A_QKV = 2 * H_A * DK_A + H_A * DV_A
IN_SIZES = (A_QKV, 2 * H_A, 2 * H_A, H_A * DV_A,
            HQ_B * DH_B, HKV_B * DH_B, HKV_B * DH_B,
            H_C * DK_C, 2 * H_C * DK_C, H_C * DV_C, H_C * DV_C,
            N_BRANCH * D_MODEL)
D_IN = sum(IN_SIZES)
IN_SPLIT_POINTS = tuple(int(v) for v in np.cumsum(IN_SIZES)[:-1])

F32 = jnp.float32

kernel_name = 'hybrid_gdn_swa_hgrn2_hmoe_diffusion'


def layer_norm(x, g, b):
    xf = x.astype(F32)
    xc = xf - jnp.mean(xf, -1, keepdims=True)
    var = jnp.mean(xc * xc, -1, keepdims=True)
    return (xc * lax.rsqrt(var + LN_EPS) * g.astype(F32) + b.astype(F32)).astype(x.dtype)


def rms_norm(x, w):
    xf = x.astype(F32)
    return xf * lax.rsqrt(jnp.mean(xf * xf, -1, keepdims=True) + RMS_EPS) * w.astype(F32)


def l2_normalize(x):
    return x * lax.rsqrt(jnp.sum(x * x, -1, keepdims=True) + L2_EPS)


def centred_depthwise_conv(x, w):
    pad = w.shape[0] // 2
    return lax.conv_general_dilated(x, w[:, None, :].astype(x.dtype), window_strides=(1,),
                                    padding=[(pad, pad)], dimension_numbers=('NWC', 'WIO', 'NWC'),
                                    feature_group_count=x.shape[-1])


def axial_rope_tables(rows):
    row = jnp.repeat(jnp.arange(rows, dtype=F32), GRID_W)
    col = jnp.broadcast_to(jnp.arange(GRID_W, dtype=F32), (rows, GRID_W)).reshape(-1)
    quarter = DH_B // 4
    inv_freq = ROPE_BASE ** (-jnp.arange(quarter, dtype=F32) / quarter)
    ang_r = row[:, None] * inv_freq
    ang_c = col[:, None] * inv_freq
    return (jnp.cos(ang_r), jnp.sin(ang_r), jnp.cos(ang_c), jnp.sin(ang_c))


def apply_axial_rope(x, tables):
    cr, sr, cc, sc = [t[None, :, None, :] for t in tables]
    r1, r2, c1, c2 = jnp.split(x.astype(F32), 4, axis=-1)
    out = jnp.concatenate([r1 * cr - r2 * sr, r2 * cr + r1 * sr,
                           c1 * cc - c2 * sc, c2 * cc + c1 * sc], -1)
    return out.astype(x.dtype)


def _to_chunks(u, size):
    bsz, nh, t = u.shape[:3]
    return jnp.moveaxis(u.reshape(bsz, nh, t // size, size, *u.shape[3:]), 2, 0)


def _masked_exp(diff, mask):
    return jnp.where(mask, jnp.exp(jnp.where(mask, diff, 0.0)), 0.0)


def gated_delta_chunked(q, k, v, beta, g, s0):
    bsz, nh, t, _ = q.shape
    dv = v.shape[-1]
    q, k, v, beta, g = [_to_chunks(u, GDN_CHUNK) for u in (q, k, v, beta, g)]
    gc = jnp.cumsum(g, axis=-1)
    pos = jnp.arange(GDN_CHUNK)
    incl = pos[:, None] >= pos[None, :]
    decay = _masked_exp(gc[..., :, None] - gc[..., None, :], incl)
    kb = k * beta[..., None]
    m = jnp.where(pos[:, None] > pos[None, :], jnp.einsum('...id,...jd->...ij', kb, k) * decay, 0.0)
    rhs = jnp.concatenate([v * beta[..., None], kb * jnp.exp(gc)[..., None]], -1)
    eye = jnp.eye(GDN_CHUNK, dtype=F32)
    uw = lax.linalg.triangular_solve(eye + m, rhs, left_side=True, lower=True, unit_diagonal=True)
    u, w = uw[..., :dv], uw[..., dv:]
    a_qk = jnp.where(incl, jnp.einsum('...id,...jd->...ij', q, k) * decay, 0.0)
    q_dec = q * jnp.exp(gc)[..., None]
    k_dec = k * jnp.exp(gc[..., -1:] - gc)[..., None]
    g_last = jnp.exp(gc[..., -1])[..., None, None]

    def step(s, xs):
        u_c, w_c, q_c, a_c, k_c, gl_c = xs
        v_new = u_c - w_c @ s
        o = q_c @ s + a_c @ v_new
        s = gl_c * s + jnp.swapaxes(k_c, -1, -2) @ v_new
        return s, o

    s_fin, o = lax.scan(step, s0, (u, w, q_dec, a_qk, k_dec, g_last))
    return jnp.moveaxis(o, 0, 2).reshape(bsz, nh, t, dv), s_fin


def gla_chunked(q, k, v, logf, s0):
    bsz, nh, t, _ = q.shape
    dv = v.shape[-1]
    q, k, v, logf = [_to_chunks(u, HGRN_CHUNK) for u in (q, k, v, logf)]
    b = jnp.cumsum(logf, axis=-2)
    pos = jnp.arange(HGRN_CHUNK)
    incl = (pos[:, None] >= pos[None, :])[:, :, None]

    def step(s, xs):
        q_c, k_c, v_c, b_c = xs
        decay = _masked_exp(b_c[..., :, None, :] - b_c[..., None, :, :], incl)
        a = jnp.einsum('bhtd,bhsd,bhtsd->bhts', q_c, k_c, decay)
        b_last = b_c[..., -1:, :]
        o = (q_c * jnp.exp(b_c)) @ s + a @ v_c
        s = jnp.exp(b_last)[..., 0, :, None] * s + jnp.swapaxes(k_c * jnp.exp(b_last - b_c), -1, -2) @ v_c
        return s, o

    s_fin, o = lax.scan(step, s0, (q, k, v, b))
    return jnp.moveaxis(o, 0, 2).reshape(bsz, nh, t, dv), s_fin


def _flip_fn(direction):
    return (lambda u: u) if direction == 0 else (lambda u: jnp.flip(u, axis=2))


def gdn_mixer(ctx_parts, lat_parts, conv_w, a_log, dt_bias, norm_w, need_ctx):
    def prep(qkv_raw, beta_logit, alpha_logit):
        bsz, t, _ = qkv_raw.shape
        qkv = jax.nn.silu(centred_depthwise_conv(qkv_raw, conv_w)).astype(F32)
        q, k, v = jnp.split(qkv, [H_A * DK_A, 2 * H_A * DK_A], -1)
        heads = lambda u, dh: u.reshape(bsz, t, H_A, dh).transpose(0, 2, 1, 3)
        q = l2_normalize(heads(q, DK_A)) * DK_A ** -0.5
        k = l2_normalize(heads(k, DK_A))
        v = heads(v, DV_A)
        per_dir = lambda u: u.astype(F32).reshape(bsz, t, 2, H_A).transpose(2, 0, 3, 1)
        beta = jax.nn.sigmoid(per_dir(beta_logit))
        g = -jnp.exp(a_log.astype(F32))[:, None, :, None] * jax.nn.softplus(
            per_dir(alpha_logit) + dt_bias.astype(F32)[:, None, :, None])
        return q, k, v, beta, g

    def finish(o, gate_raw):
        bsz, _, t, _ = o.shape
        o = jnp.transpose(rms_norm(o, norm_w), (0, 2, 1, 3)).reshape(bsz, t, H_A * DV_A)
        return (o * jax.nn.silu(gate_raw.astype(F32))).astype(gate_raw.dtype)

    qc, kc, vc, beta_c, g_c = prep(*ctx_parts[:3])
    ql, kl, vl, beta_l, g_l = prep(*lat_parts[:3])
    oc = 0.0
    ol = 0.0
    for direction in range(2):
        f = _flip_fn(direction)
        s0 = jnp.zeros((qc.shape[0], H_A, DK_A, DV_A), F32)
        o_c, s_ctx = gated_delta_chunked(f(qc), f(kc), f(vc), f(beta_c[direction]), f(g_c[direction]), s0)
        o_l, _ = gated_delta_chunked(f(ql), f(kl), f(vl), f(beta_l[direction]), f(g_l[direction]), s_ctx)
        oc = oc + f(o_c)
        ol = ol + f(o_l)
    yl = finish(ol, lat_parts[3])
    yc = finish(oc, ctx_parts[3]) if need_ctx else None
    return yc, yl


def window_attention_mixer(ctx_parts, lat_parts, rope, sink, need_ctx):
    qc_raw, kc_raw, vc_raw = ctx_parts
    ql_raw, kl_raw, vl_raw = lat_parts
    bsz, n_lat, _ = ql_raw.shape
    n_ctx = kc_raw.shape[1]
    nb = n_lat // WIN_BLOCK
    scale = DH_B ** -0.5
    ql = apply_axial_rope(ql_raw.reshape(bsz, n_lat, HQ_B, DH_B), rope)
    kl = apply_axial_rope(kl_raw.reshape(bsz, n_lat, HKV_B, DH_B), rope)
    vl = vl_raw.reshape(bsz, n_lat, HKV_B, DH_B)
    kc = kc_raw.reshape(bsz, n_ctx, HKV_B, DH_B)
    vc = vc_raw.reshape(bsz, n_ctx, HKV_B, DH_B)
    sink_f = sink.astype(F32).reshape(HKV_B, GROUP_B)

    qb = ql.reshape(bsz, nb, WIN_BLOCK, HKV_B, GROUP_B, DH_B)

    def band(u):
        up = jnp.pad(u, ((0, 0), (WIN_BLOCK, WIN_BLOCK), (0, 0), (0, 0))).reshape(bsz, nb + 2, WIN_BLOCK, HKV_B, DH_B)
        return jnp.concatenate([up[:, :-2], up[:, 1:-1], up[:, 2:]], axis=2)

    kw, vw = band(kl), band(vl)
    s_win = jnp.einsum('bnqhgd,bnkhd->bnhgqk', qb, kw).astype(F32) * scale
    q_pos = jnp.arange(n_lat).reshape(nb, WIN_BLOCK)
    k_pos = (jnp.arange(nb)[:, None] - 1) * WIN_BLOCK + jnp.arange(3 * WIN_BLOCK)[None, :]
    allowed = ((jnp.abs(q_pos[:, :, None] - k_pos[:, None, :]) <= WINDOW)
               & (k_pos[:, None, :] >= 0) & (k_pos[:, None, :] < n_lat))
    s_win = jnp.where(allowed[None, :, None, None], s_win, MASK_VALUE)
    s_ctx = jnp.einsum('bnqhgd,bchd->bnhgqc', qb, kc).astype(F32) * scale
    sink_b = sink_f[None, None, :, :, None, None]
    m = jnp.maximum(jnp.maximum(s_win.max(-1, keepdims=True), s_ctx.max(-1, keepdims=True)), sink_b)
    p_win = jnp.exp(s_win - m)
    p_ctx = jnp.exp(s_ctx - m)
    inv = 1.0 / (p_win.sum(-1, keepdims=True) + p_ctx.sum(-1, keepdims=True) + jnp.exp(sink_b - m))
    ol = (jnp.einsum('bnhgqk,bnkhd->bnqhgd', (p_win * inv).astype(vw.dtype), vw)
          + jnp.einsum('bnhgqc,bchd->bnqhgd', (p_ctx * inv).astype(vc.dtype), vc))
    yl = ol.reshape(bsz, n_lat, HQ_B * DH_B).astype(ql_raw.dtype)

    yc = None
    if need_ctx:
        qcb = qc_raw.reshape(bsz, n_ctx, HKV_B, GROUP_B, DH_B)
        s = jnp.einsum('bqhgd,bkhd->bhgqk', qcb, kc).astype(F32) * scale
        sink_c = sink_f[None, :, :, None, None]
        mc = jnp.maximum(s.max(-1, keepdims=True), sink_c)
        p = jnp.exp(s - mc)
        p = p / (p.sum(-1, keepdims=True) + jnp.exp(sink_c - mc))
        oc = jnp.einsum('bhgqk,bkhd->bqhgd', p.astype(vc.dtype), vc)
        yc = oc.reshape(bsz, n_ctx, HQ_B * DH_B).astype(qc_raw.dtype)
    return yc, yl


def hgrn_mixer(ctx_parts, lat_parts, lower_bound, norm_w, need_ctx):
    lbf = lower_bound.astype(F32)

    def prep(q_raw, f_raw, i_raw):
        bsz, t, _ = q_raw.shape
        heads = lambda u: u.reshape(bsz, t, H_C, -1).transpose(0, 2, 1, 3)
        q = heads(jax.nn.silu(q_raw.astype(F32)))
        v = heads(i_raw.astype(F32))
        x = f_raw.astype(F32).reshape(bsz, t, 2, H_C * DK_C)
        logf = jnp.log(jnp.maximum(lbf + (1.0 - lbf) * jax.nn.sigmoid(x), F_FLOOR))
        k = (1.0 - lbf) * jax.nn.sigmoid(-x)
        per_dir = lambda u: u.reshape(bsz, t, 2, H_C, DK_C).transpose(2, 0, 3, 1, 4)
        return q, per_dir(k), v, per_dir(logf)

    def finish(o, gate_raw):
        bsz, _, t, _ = o.shape
        o = jnp.transpose(rms_norm(o, norm_w), (0, 2, 1, 3)).reshape(bsz, t, H_C * DV_C)
        return (o * jax.nn.silu(gate_raw.astype(F32))).astype(gate_raw.dtype)

    qc, kc, vc, logf_c = prep(*ctx_parts[:3])
    ql, kl, vl, logf_l = prep(*lat_parts[:3])
    oc = 0.0
    ol = 0.0
    for direction in range(2):
        f = _flip_fn(direction)
        s0 = jnp.zeros((qc.shape[0], H_C, DK_C, DV_C), F32)
        o_c, s_ctx = gla_chunked(f(qc), f(kc[direction]), f(vc), f(logf_c[direction]), s0)
        o_l, _ = gla_chunked(f(ql), f(kl[direction]), f(vl), f(logf_l[direction]), s_ctx)
        oc = oc + f(o_c)
        ol = ol + f(o_l)
    yl = finish(ol, lat_parts[3])
    yc = finish(oc, ctx_parts[3]) if need_ctx else None
    return yc, yl


def branch_merge(ya, yb, yc, gate_logits, wa, wb, wc, wo):
    gate_a, gate_b, gate_c = jnp.split(jax.nn.sigmoid(gate_logits), N_BRANCH, -1)
    merged = gate_a * (ya @ wa) + gate_b * (yb @ wb) + gate_c * (yc @ wc)
    return merged @ wo


def hierarchical_moe(h, w_group, b_group, w_router, b_router, w1, w3, w2):
    n_tok, d = h.shape
    group_logits = (h @ w_group).astype(F32) + b_group.astype(F32)
    group_idx = jnp.argmax(group_logits, axis=-1).astype(jnp.int32)
    group_w = jnp.take_along_axis(jax.nn.softmax(group_logits, -1), group_idx[:, None], -1)
    exp_logits = ((h @ w_router).astype(F32) + b_router.astype(F32)).reshape(n_tok, N_GROUPS, EXPERTS_PER_GROUP)
    in_group = jnp.take_along_axis(exp_logits, group_idx[:, None, None], 1)[:, 0]
    top_logit, top_idx = lax.top_k(in_group, TOP_K_IN_GROUP)
    weights = jax.nn.softmax(top_logit, -1) * group_w
    expert_id = group_idx[:, None] * EXPERTS_PER_GROUP + top_idx.astype(jnp.int32)

    n_assign = n_tok * TOP_K_IN_GROUP
    flat_e = expert_id.reshape(-1)
    flat_tok = jnp.repeat(jnp.arange(n_tok, dtype=jnp.int32), TOP_K_IN_GROUP)
    order = jnp.argsort(flat_e)
    sorted_e = flat_e[order]
    counts = jnp.zeros((N_EXPERTS,), jnp.int32).at[flat_e].add(1)
    padded = (counts + MOE_BLOCK - 1) // MOE_BLOCK * MOE_BLOCK
    pad_end = jnp.cumsum(padded)
    pad_start = pad_end - padded
    start = jnp.cumsum(counts) - counts
    dest = pad_start[sorted_e] + jnp.arange(n_assign, dtype=jnp.int32) - start[sorted_e]
    n_blocks = -(-n_assign // MOE_BLOCK) + N_EXPERTS
    cap = n_blocks * MOE_BLOCK
    slot_tok = jnp.full((cap,), n_tok, jnp.int32).at[dest].set(flat_tok[order])
    slot_w = jnp.zeros((cap,), F32).at[dest].set(weights.reshape(-1)[order])
    block_expert = jnp.minimum(jnp.searchsorted(pad_end, jnp.arange(n_blocks, dtype=jnp.int32) * MOE_BLOCK,
                                                side='right'), N_EXPERTS - 1)
    h_pad = jnp.concatenate([h, jnp.zeros((1, d), h.dtype)], 0)

    def run_block(args):
        tok, e = args
        xb = h_pad[tok]
        hid = jax.nn.silu(xb @ w1[e]) * (xb @ w3[e])
        return hid @ w2[e]

    y = lax.map(run_block, (slot_tok.reshape(n_blocks, MOE_BLOCK), block_expert))
    y = y.reshape(cap, d).astype(F32) * slot_w[:, None]
    return jnp.zeros((n_tok + 1, d), F32).at[slot_tok].add(y)[:n_tok].astype(h.dtype)


def setup_inputs(seed: int = 0) -> dict:
    key = jax.random.key(seed)
    ks = list(jax.random.split(key, 32))

    def nrm(i, shape, scale):
        return jax.random.normal(ks[i], shape, F32) * scale

    d = D_MODEL
    dt = jnp.exp(jax.random.uniform(ks[8], (DEPTH, 2, H_A), F32, math.log(1e-3), math.log(1e-1)))
    return {
        'x': nrm(0, (BATCH, SEQ, d), 1.0),
        'c': nrm(1, (BATCH, d), 1.0),
        'ctx': nrm(2, (BATCH, CTX_LEN, d), 1.0),
        'c_ctx': nrm(3, (d,), 1.0),
        'w_mod': nrm(4, (DEPTH, d, 6 * d), d ** -0.5),
        'b_mod': nrm(5, (DEPTH, 6 * d), 0.02),
        'w_in': nrm(6, (DEPTH, d, D_IN), d ** -0.5),
        'conv_a': nrm(7, (DEPTH, CONV_K, A_QKV), CONV_K ** -0.5),
        'gdn_a_log': jnp.log(jax.random.uniform(ks[9], (DEPTH, 2, H_A), F32, 1.0, 16.0)),
        'gdn_dt_bias': dt + jnp.log(-jnp.expm1(-dt)),
        'gdn_norm_w': 1.0 + nrm(10, (DEPTH, DV_A), 0.02),
        'attn_sink': nrm(11, (DEPTH, HQ_B), 0.5),
        'hgrn_lb_logits': nrm(12, (2, DEPTH, H_C * DK_C), 0.1),
        'hgrn_norm_w': 1.0 + nrm(13, (DEPTH, DV_C), 0.02),
        'w_branch_a': nrm(14, (DEPTH, H_A * DV_A, d), (H_A * DV_A) ** -0.5 * DEEPNORM_BETA),
        'w_branch_b': nrm(15, (DEPTH, HQ_B * DH_B, d), (HQ_B * DH_B) ** -0.5 * DEEPNORM_BETA),
        'w_branch_c': nrm(16, (DEPTH, H_C * DV_C, d), (H_C * DV_C) ** -0.5 * DEEPNORM_BETA),
        'w_out': nrm(17, (DEPTH, d, d), d ** -0.5 * DEEPNORM_BETA),
        'ln1_g': 1.0 + nrm(18, (DEPTH, d), 0.02),
        'ln1_b': nrm(19, (DEPTH, d), 0.02),
        'ln2_g': 1.0 + nrm(20, (DEPTH, d), 0.02),
        'ln2_b': nrm(21, (DEPTH, d), 0.02),
        'w_group': nrm(22, (DEPTH, d, N_GROUPS), d ** -0.5),
        'b_group': nrm(23, (DEPTH, N_GROUPS), 0.01),
        'w_router': nrm(24, (DEPTH, d, N_EXPERTS), d ** -0.5),
        'b_router': nrm(25, (DEPTH, N_EXPERTS), 0.01),
        'w1': nrm(26, (DEPTH, N_EXPERTS, d, D_EXPERT), d ** -0.5),
        'w3': nrm(27, (DEPTH, N_EXPERTS, d, D_EXPERT), d ** -0.5),
        'w2': nrm(28, (DEPTH, N_EXPERTS, D_EXPERT, d), D_EXPERT ** -0.5 * DEEPNORM_BETA),
    }


def reference(x, c, ctx, c_ctx, w_mod, b_mod, w_in, conv_a, gdn_a_log, gdn_dt_bias, gdn_norm_w,
              attn_sink, hgrn_lb_logits, hgrn_norm_w, w_branch_a, w_branch_b, w_branch_c, w_out,
              ln1_g, ln1_b, ln2_g, ln2_b, w_group, b_group, w_router, b_router, w1, w3, w2):
    bsz, n_lat, d = x.shape
    rows = n_lat // GRID_W
    rope = axial_rope_tables(rows)
    lb_soft = jax.nn.softmax(hgrn_lb_logits.astype(F32), axis=1)
    lower_bounds = jnp.cumsum(lb_soft, axis=1) - lb_soft[:, :1]

    xl, xc = x, ctx
    for layer in range(DEPTH):
        need_ctx = layer < DEPTH - 1
        mod_l = jax.nn.silu(c) @ w_mod[layer] + b_mod[layer]
        mod_c = jax.nn.silu(c_ctx) @ w_mod[layer] + b_mod[layer]
        sh1_l, sc1_l, g1_l, sh2_l, sc2_l, g2_l = jnp.split(mod_l[:, None, :], 6, -1)
        sh1_c, sc1_c, g1_c, sh2_c, sc2_c, g2_c = jnp.split(mod_c, 6, -1)

        hl = xl * (1.0 + sc1_l) + sh1_l
        hc = xc * (1.0 + sc1_c) + sh1_c
        pl = jnp.split(hl @ w_in[layer], IN_SPLIT_POINTS, -1)
        pc = jnp.split(hc @ w_in[layer], IN_SPLIT_POINTS, -1)
        ya_c, ya_l = gdn_mixer(pc[0:4], pl[0:4], conv_a[layer], gdn_a_log[layer], gdn_dt_bias[layer],
                               gdn_norm_w[layer], need_ctx)
        yb_c, yb_l = window_attention_mixer(pc[4:7], pl[4:7], rope, attn_sink[layer], need_ctx)
        yc_c, yc_l = hgrn_mixer(pc[7:11], pl[7:11], lower_bounds[:, layer], hgrn_norm_w[layer], need_ctx)
        out_l = branch_merge(ya_l, yb_l, yc_l, pl[11], w_branch_a[layer], w_branch_b[layer],
                             w_branch_c[layer], w_out[layer])
        xl = layer_norm(DEEPNORM_ALPHA * xl + g1_l * out_l, ln1_g[layer], ln1_b[layer])
        if need_ctx:
            out_c = branch_merge(ya_c, yb_c, yc_c, pc[11], w_branch_a[layer], w_branch_b[layer],
                                 w_branch_c[layer], w_out[layer])
            xc = layer_norm(DEEPNORM_ALPHA * xc + g1_c * out_c, ln1_g[layer], ln1_b[layer])

        h2l = (xl * (1.0 + sc2_l) + sh2_l).reshape(-1, d)
        if need_ctx:
            h2c = (xc * (1.0 + sc2_c) + sh2_c).reshape(-1, d)
            y = hierarchical_moe(jnp.concatenate([h2l, h2c], 0), w_group[layer], b_group[layer],
                                 w_router[layer], b_router[layer], w1[layer], w3[layer], w2[layer])
            y_l = y[:h2l.shape[0]].reshape(xl.shape)
            y_c = y[h2l.shape[0]:].reshape(xc.shape)
            xc = layer_norm(DEEPNORM_ALPHA * xc + g2_c * y_c, ln2_g[layer], ln2_b[layer])
        else:
            y_l = hierarchical_moe(h2l, w_group[layer], b_group[layer], w_router[layer], b_router[layer],
                                   w1[layer], w3[layer], w2[layer]).reshape(xl.shape)
        xl = layer_norm(DEEPNORM_ALPHA * xl + g2_l * y_l, ln2_g[layer], ln2_b[layer])
    return xl
```

```python
import functools
import math

import numpy as np
import jax
import jax.numpy as jnp
from jax import lax
from jax.experimental import pallas as pl
from jax.experimental.pallas import tpu as pltpu

F32 = jnp.float32
BF16 = jnp.bfloat16
HI = lax.Precision.HIGHEST

D_MODEL = 1024
GRID_W = 64
H_A, DK_A, DV_A, CONV_K = 4, 128, 128, 5
HQ_B, HKV_B, DH_B = 8, 2, 64
GROUP_B = HQ_B // HKV_B
WIN_BLOCK = 128
ROPE_BASE = 10000.0
H_C, DK_C, DV_C = 4, 128, 128
CHUNK = 64
SUB = 16
N_GROUPS, EXPERTS_PER_GROUP, TOP_K = 4, 8, 2
N_EXPERTS = N_GROUPS * EXPERTS_PER_GROUP
D_EXPERT = 512
LN_EPS, RMS_EPS, L2_EPS = 1e-5, 1e-6, 1e-6
MASK_VALUE = -1e30
F_FLOOR = 1e-30

LANE = 128
SUBLANE = 8
VMEM_LIMIT = 56 * 1024 * 1024

A_QKV = 2 * H_A * DK_A + H_A * DV_A
OFF_QKV_A, OFF_GATE_A, OFF_Q_B, OFF_Q_C, OFF_F_C, OFF_I_C, OFF_GATE_C, OFF_GATES, OFF_KV_B, OFF_SMALL = (
    0, 1536, 2048, 2560, 3072, 4096, 4608, 5120, 8192, 8448)
PROJ_W = 8704
PROJ_TN = PROJ_W // 4

TM = 256
TMB = 256


def _cparams(sem):
    return pltpu.CompilerParams(dimension_semantics=sem, vmem_limit_bytes=VMEM_LIMIT)


def _sigmoid(x):
    return 1.0 / (1.0 + jnp.exp(-x))


def _silu(x):
    return x * _sigmoid(x)


def _softplus(x):
    return jnp.maximum(x, 0.0) + jnp.log(1.0 + jnp.exp(-jnp.abs(x)))


def _dot(a, b):
    return jnp.dot(a.astype(BF16), b.astype(BF16), preferred_element_type=F32)


def _dot_nt(a, b):
    return lax.dot_general(a.astype(BF16), b.astype(BF16), (((1,), (1,)), ((), ())),
                           preferred_element_type=F32)


def _dot_tn(a, b):
    return lax.dot_general(a.astype(BF16), b.astype(BF16), (((0,), (0,)), ((), ())),
                           preferred_element_type=F32)


def _dot_hi(a, b):
    return jnp.dot(a, b, precision=HI, preferred_element_type=F32)


def _tri(n, reverse):
    i = lax.broadcasted_iota(jnp.int32, (n, n), 0)
    j = lax.broadcasted_iota(jnp.int32, (n, n), 1)
    return jnp.where((j >= i) if reverse else (j <= i), 1.0, 0.0).astype(F32)


def _mod_kernel(c_ref, w_ref, b_ref, o_ref):
    o_ref[0] = _dot(_silu(c_ref[...]), w_ref[0]) + b_ref[0]


def modulation(c_rows, w_mod, b_mod):
    depth, d, n6 = w_mod.shape
    tn = n6 // 4
    return pl.pallas_call(
        _mod_kernel,
        grid=(depth, n6 // tn),
        in_specs=[pl.BlockSpec((SUBLANE, d), lambda l, n: (0, 0)),
                  pl.BlockSpec((1, d, tn), lambda l, n: (l, 0, n)),
                  pl.BlockSpec((1, 1, tn), lambda l, n: (l, 0, n))],
        out_specs=pl.BlockSpec((1, SUBLANE, tn), lambda l, n: (l, 0, n)),
        out_shape=jax.ShapeDtypeStruct((depth, SUBLANE, n6), F32),
        compiler_params=_cparams(("parallel", "parallel")),
        name="modulation",
    )(c_rows, w_mod, b_mod.reshape(depth, 1, n6))


def _inproj_kernel(x_ref, sc_ref, sh_ref, w_ref, o_ref):
    h = x_ref[...] * (1.0 + sc_ref[0]) + sh_ref[0]
    o_ref[...] = _dot(h, w_ref[...])


def _small_t_kernel(x_ref, sc_ref, sh_ref, wt_ref, o_ref):
    h = x_ref[...] * (1.0 + sc_ref[0]) + sh_ref[0]
    o_ref[...] = _dot_nt(wt_ref[...], h)


def _row_sel(bsz, tiles_per_batch, ctx_tiles):
    def sel(m):
        return jnp.where(m % tiles_per_batch < ctx_tiles, bsz, m // tiles_per_batch)
    return sel


def in_projection(x, sc, sh, w_perm, w_small_t, sel):
    nt, d = x.shape
    n_m = nt // TM
    proj = pl.pallas_call(
        _inproj_kernel,
        grid=(PROJ_W // PROJ_TN, n_m),
        in_specs=[pl.BlockSpec((TM, d), lambda n, m: (m, 0)),
                  pl.BlockSpec((1, 1, d), lambda n, m: (sel(m), 0, 0)),
                  pl.BlockSpec((1, 1, d), lambda n, m: (sel(m), 0, 0)),
                  pl.BlockSpec((d, PROJ_TN), lambda n, m: (0, n))],
        out_specs=pl.BlockSpec((TM, PROJ_TN), lambda n, m: (m, n)),
        out_shape=jax.ShapeDtypeStruct((nt, PROJ_W), F32),
        compiler_params=_cparams(("parallel", "parallel")),
        name="in_projection",
    )(x, sc, sh, w_perm)
    small_t = pl.pallas_call(
        _small_t_kernel,
        grid=(n_m,),
        in_specs=[pl.BlockSpec((TM, d), lambda m: (m, 0)),
                  pl.BlockSpec((1, 1, d), lambda m: (sel(m), 0, 0)),
                  pl.BlockSpec((1, 1, d), lambda m: (sel(m), 0, 0)),
                  pl.BlockSpec((2 * SUBLANE, d), lambda m: (0, 0))],
        out_specs=pl.BlockSpec((2 * SUBLANE, TM), lambda m: (0, m)),
        out_shape=jax.ShapeDtypeStruct((2 * SUBLANE, nt), F32),
        compiler_params=_cparams(("parallel",)),
        name="small_projection_t",
    )(x, sc, sh, w_small_t)
    return proj, small_t


def _chunk_of(j, ncc, nch, reverse):
    if not reverse:
        return j
    return jnp.where(j < ncc, ncc - 1 - j, nch - 1 - j + ncc)


def _gdn_kernel(reverse, ncc, nch, cur_ref, prev_ref, next_ref, scol_ref, srow_ref, convw_ref,
                arow_ref, dtrow_ref, acol_ref, dtcol_ref, o_ref, s_ref, xw_ref):
    d = 1 if reverse else 0
    j = pl.program_id(1)
    c = _chunk_of(j, ncc, nch, reverse)

    @pl.when(j == 0)
    def _():
        s_ref[...] = jnp.zeros_like(s_ref)

    has_prev = jnp.logical_and(c != 0, c != ncc)
    has_next = jnp.logical_and(c != ncc - 1, c != nch - 1)
    xw_ref[0:SUBLANE, :] = jnp.where(has_prev, prev_ref[...], 0.0)
    xw_ref[SUBLANE:SUBLANE + CHUNK, :] = cur_ref[...]
    xw_ref[SUBLANE + CHUNK:2 * SUBLANE + CHUNK, :] = jnp.where(has_next, next_ref[...], 0.0)
    pad = CONV_K // 2
    y = convw_ref[0:1, :] * xw_ref[SUBLANE - pad:SUBLANE - pad + CHUNK, :]
    for t in range(1, CONV_K):
        y = y + convw_ref[t:t + 1, :] * xw_ref[SUBLANE - pad + t:SUBLANE - pad + t + CHUNK, :]
    qkv = _silu(y)

    tri = _tri(CHUNK, reverse)
    scol = scol_ref[...]
    srow = srow_ref[0]
    beta_col = _sigmoid(scol)
    g_col = -jnp.exp(arow_ref[...]) * _softplus(scol + dtrow_ref[...])
    g_row = -jnp.exp(acol_ref[...]) * _softplus(srow + dtcol_ref[...])
    gc_col = _dot_hi(tri, g_col)
    gc_row = lax.dot_general(g_row, tri, (((1,), (1,)), ((), ())), precision=HI,
                             preferred_element_type=F32)
    last = 0 if reverse else CHUNK - 1

    ii = lax.broadcasted_iota(jnp.int32, (CHUNK, CHUNK), 0)
    jj = lax.broadcasted_iota(jnp.int32, (CHUNK, CHUNK), 1)
    incl = (ii <= jj) if reverse else (ii >= jj)
    strict = (ii < jj) if reverse else (ii > jj)
    eye = jnp.where(ii == jj, 1.0, 0.0).astype(F32)

    outs = []
    for h in range(H_A):
        q = qkv[:, h * DK_A:(h + 1) * DK_A]
        k = qkv[:, H_A * DK_A + h * DK_A:H_A * DK_A + (h + 1) * DK_A]
        v = qkv[:, 2 * H_A * DK_A + h * DV_A:2 * H_A * DK_A + (h + 1) * DV_A]
        q = q * lax.rsqrt(jnp.sum(q * q, -1, keepdims=True) + L2_EPS) * (DK_A ** -0.5)
        k = k * lax.rsqrt(jnp.sum(k * k, -1, keepdims=True) + L2_EPS)
        lb = d * H_A + h
        la = 2 * H_A + d * H_A + h
        beta = beta_col[:, lb:lb + 1]
        gcc = gc_col[:, la:la + 1]
        gcr = gc_row[la:la + 1, :]
        gc_last = gcc[last:last + 1, :]
        decay = jnp.where(incl, jnp.exp(jnp.where(incl, gcc - gcr, 0.0)), 0.0)
        qk_kk = _dot_nt(jnp.concatenate([q, k], axis=0), k)
        a_qk = jnp.where(incl, qk_kk[:CHUNK] * decay, 0.0)
        m_neg = -jnp.where(strict, qk_kk[CHUNK:] * beta * decay, 0.0)
        t_inv = eye + m_neg
        p = m_neg
        for _ in range(5):
            p = _dot_hi(p, p)
            t_inv = t_inv + _dot_hi(t_inv, p)
        kb = k * beta
        rhs = jnp.concatenate([v * beta, kb * jnp.exp(gcc)], axis=1)
        uw = _dot_hi(t_inv, rhs)
        u, w = uw[:, :DV_A], uw[:, DV_A:]
        s = s_ref[h]
        wq_s = _dot(jnp.concatenate([w, q * jnp.exp(gcc)], axis=0), s)
        v_new = u - wq_s[:CHUNK]
        outs.append(wq_s[CHUNK:] + _dot(a_qk, v_new))
        k_dec = k * jnp.exp(gc_last - gcc)
        s_ref[h] = jnp.exp(gc_last) * s + _dot_tn(k_dec, v_new)
    o_ref[...] = jnp.concatenate(outs, axis=1)


def gdn_direction(proj, small_rows, conv_w, a_log, dt_bias, bsz, ncc, nch, reverse):
    nt = proj.shape[0]
    cps = CHUNK // SUBLANE
    d = 1 if reverse else 0
    arow = jnp.zeros((1, LANE), F32).at[0, 2 * H_A:4 * H_A].set(a_log.reshape(-1))
    dtrow = jnp.zeros((1, LANE), F32).at[0, 2 * H_A:4 * H_A].set(dt_bias.reshape(-1))
    acol = arow[0, :2 * SUBLANE].reshape(2 * SUBLANE, 1)
    dtcol = dtrow[0, :2 * SUBLANE].reshape(2 * SUBLANE, 1)
    chunk = functools.partial(_chunk_of, ncc=ncc, nch=nch, reverse=reverse)
    n8 = nt // SUBLANE
    full = lambda shape: pl.BlockSpec(shape, lambda b, j: (0,) * len(shape))
    return pl.pallas_call(
        functools.partial(_gdn_kernel, reverse, ncc, nch),
        grid=(bsz, nch),
        in_specs=[
            pl.BlockSpec((CHUNK, A_QKV), lambda b, j: (b * nch + chunk(j), 0)),
            pl.BlockSpec((SUBLANE, A_QKV), lambda b, j: (jnp.maximum((b * nch + chunk(j)) * cps - 1, 0), 0)),
            pl.BlockSpec((SUBLANE, A_QKV), lambda b, j: (jnp.minimum((b * nch + chunk(j) + 1) * cps, n8 - 1), 0)),
            pl.BlockSpec((CHUNK, LANE), lambda b, j: (b * nch + chunk(j), OFF_SMALL // LANE)),
            pl.BlockSpec((1, 2 * SUBLANE, CHUNK), lambda b, j: (b * nch + chunk(j), 0, 0)),
            full((CONV_K, A_QKV)), full((1, LANE)), full((1, LANE)),
            full((2 * SUBLANE, 1)), full((2 * SUBLANE, 1)),
        ],
        out_specs=pl.BlockSpec((CHUNK, H_A * DV_A), lambda b, j: (b * nch + chunk(j), 0)),
        out_shape=jax.ShapeDtypeStruct((nt, H_A * DV_A), F32),
        scratch_shapes=[pltpu.VMEM((H_A, DK_A, DV_A), F32),
                        pltpu.VMEM((CHUNK + 2 * SUBLANE, A_QKV), F32)],
        compiler_params=_cparams(("parallel", "arbitrary")),
        name="gdn_bwd" if reverse else "gdn_fwd",
    )(proj, proj, proj, proj, small_rows, conv_w, arow, dtrow, acol, dtcol)


def _hgrn_kernel(reverse, q_ref, f_ref, i_ref, lb_ref, o_ref, st_ref):
    j = pl.program_id(1)

    @pl.when(j == 0)
    def _():
        st_ref[...] = jnp.zeros_like(st_ref)

    lb = lb_ref[...]
    x = f_ref[...]
    logf = jnp.log(jnp.maximum(lb + (1.0 - lb) * _sigmoid(x), F_FLOOR))
    k_all = (1.0 - lb) * _sigmoid(-x)
    q_all = _silu(q_ref[...])
    v_all = i_ref[...]
    b_all = _dot_hi(_tri(CHUNK, reverse), logf)
    last = 0 if reverse else CHUNK - 1
    lane = lax.broadcasted_iota(jnp.int32, (SUB, CHUNK), 1)
    trow = lax.broadcasted_iota(jnp.int32, (SUB, 1), 0)
    nsub = CHUNK // SUB

    outs = []
    for h in range(H_C):
        sl = slice(h * DK_C, (h + 1) * DK_C)
        b, q, k, v = b_all[:, sl], q_all[:, sl], k_all[:, sl], v_all[:, h * DV_C:(h + 1) * DV_C]
        rows = []
        for blk in range(nsub):
            r0 = blk * SUB
            b_i, q_i = b[r0:r0 + SUB], q[r0:r0 + SUB]
            acc = jnp.zeros((SUB, CHUNK), F32)
            for sl_ in range(SUB):
                s = r0 + sl_
                ok = (trow <= sl_) if reverse else (trow >= sl_)
                e = jnp.where(ok, jnp.exp(jnp.where(ok, b_i - b[s:s + 1], 0.0)), 0.0)
                col = jnp.sum(q_i * k[s:s + 1] * e, axis=-1, keepdims=True)
                acc = jnp.where(lane == s, col, acc)
            has_off = (blk < nsub - 1) if reverse else (blk > 0)
            if has_off:
                b_ref = b[r0 + SUB:r0 + SUB + 1] if reverse else b[r0 - 1:r0]
                q_off = q_i * jnp.exp(b_i - b_ref)
                k_off = k * jnp.exp(jnp.minimum(b_ref - b, 0.0))
                a_off = _dot_nt(q_off, k_off)
                earlier = (lane >= r0 + SUB) if reverse else (lane < r0)
                acc = jnp.where(earlier, a_off, acc)
            rows.append(acc)
        a = jnp.concatenate(rows, axis=0)
        st = st_ref[h]
        b_last = b[last:last + 1]
        outs.append(_dot_nt(q * jnp.exp(b), st) + _dot(a, v))
        st_ref[h] = jnp.exp(b_last) * st + _dot_tn(v, k * jnp.exp(b_last - b))
    o_ref[...] = jnp.concatenate(outs, axis=1)


def hgrn_direction(proj, lower_bound, bsz, ncc, nch, reverse):
    nt = proj.shape[0]
    d = 1 if reverse else 0
    w = H_C * DK_C
    chunk = functools.partial(_chunk_of, ncc=ncc, nch=nch, reverse=reverse)
    return pl.pallas_call(
        functools.partial(_hgrn_kernel, reverse),
        grid=(bsz, nch),
        in_specs=[
            pl.BlockSpec((CHUNK, w), lambda b, j: (b * nch + chunk(j), OFF_Q_C // w)),
            pl.BlockSpec((CHUNK, w), lambda b, j: (b * nch + chunk(j), OFF_F_C // w + d)),
            pl.BlockSpec((CHUNK, w), lambda b, j: (b * nch + chunk(j), OFF_I_C // w)),
            pl.BlockSpec((1, w), lambda b, j: (0, 0)),
        ],
        out_specs=pl.BlockSpec((CHUNK, H_C * DV_C), lambda b, j: (b * nch + chunk(j), 0)),
        out_shape=jax.ShapeDtypeStruct((nt, H_C * DV_C), F32),
        scratch_shapes=[pltpu.VMEM((H_C, DV_C, DK_C), F32)],
        compiler_params=_cparams(("parallel", "arbitrary")),
        name="hgrn_bwd" if reverse else "hgrn_fwd",
    )(proj, proj, proj, lower_bound[d].reshape(1, w))


def _rope(x, cos, sin):
    n = x.shape[-1]
    lane = lax.broadcasted_iota(jnp.int32, x.shape, 1)
    partner = jnp.where(lane % 32 < 16, pltpu.roll(x, n - 16, 1), pltpu.roll(x, 16, 1))
    return x * cos + partner * sin


def _attn_kernel(ctx_blocks, n_qb, sink_ref, q_ref, kv_own_ref, kv_prev_ref, kv_next_ref, kv_ctx_ref,
                 cos_ref, sin_ref, cos_p_ref, sin_p_ref, cos_n_ref, sin_n_ref, o_ref):
    qb = pl.program_id(1)
    wb = WIN_BLOCK
    kw = HKV_B * DH_B
    scale = DH_B ** -0.5
    cos, sin = cos_ref[...], sin_ref[...]
    q = _rope(q_ref[...], jnp.concatenate([cos] * (HQ_B * DH_B // LANE), axis=1),
              jnp.concatenate([sin] * (HQ_B * DH_B // LANE), axis=1))
    k_own = _rope(kv_own_ref[:, :kw], cos, sin)
    k_prev = _rope(kv_prev_ref[:, :kw], cos_p_ref[...], sin_p_ref[...])
    k_next = _rope(kv_next_ref[:, :kw], cos_n_ref[...], sin_n_ref[...])
    k_all = jnp.concatenate([k_prev, k_own, k_next, kv_ctx_ref[:, :kw]], axis=0)
    v_all = jnp.concatenate([kv_prev_ref[:, kw:], kv_own_ref[:, kw:], kv_next_ref[:, kw:],
                             kv_ctx_ref[:, kw:]], axis=0)
    n_keys = k_all.shape[0]

    is_lat = qb >= ctx_blocks
    prev_ok = qb >= ctx_blocks + 1
    next_ok = jnp.logical_and(is_lat, qb <= n_qb - 2)
    i = lax.broadcasted_iota(jnp.int32, (wb, n_keys), 0)
    jcol = lax.broadcasted_iota(jnp.int32, (wb, n_keys), 1)
    never = 4 * n_keys
    off_prev = jnp.where(prev_ok, 0, never)
    off_own = jnp.where(is_lat, 0, never)
    off_next = jnp.where(next_ok, 0, never)
    allowed = (((jcol < wb) & (jcol >= i + off_prev))
               | ((jcol >= wb + off_own) & (jcol < 2 * wb))
               | ((jcol >= 2 * wb) & (jcol < 3 * wb) & (jcol - 2 * wb + off_next <= i))
               | (jcol >= 3 * wb))

    outs = []
    for hq in range(HQ_B):
        g = hq // GROUP_B
        qh = q[:, hq * DH_B:(hq + 1) * DH_B]
        s = _dot_nt(qh, k_all[:, g * DH_B:(g + 1) * DH_B]) * scale
        s = jnp.where(allowed, s, MASK_VALUE)
        sink = sink_ref[hq]
        m = jnp.maximum(jnp.max(s, axis=-1, keepdims=True), sink)
        p = jnp.exp(s - m)
        inv = 1.0 / (jnp.sum(p, axis=-1, keepdims=True) + jnp.exp(sink - m))
        outs.append(_dot(p, v_all[:, g * DH_B:(g + 1) * DH_B]) * inv)
    o_ref[...] = jnp.concatenate(outs, axis=1)


def window_attention(proj, sink, cos_t, sin_t, bsz, tt, tc):
    nt = proj.shape[0]
    wb = WIN_BLOCK
    n_qb = tt // wb
    ctx_blocks = tc // wb
    kvw = 2 * HKV_B * DH_B
    kv_col = OFF_KV_B // kvw
    blk = lambda b, qb: b * n_qb + qb
    prev = lambda qb: jnp.maximum(qb - 1, 0)
    nxt = lambda qb: jnp.minimum(qb + 1, n_qb - 1)
    return pl.pallas_call(
        functools.partial(_attn_kernel, ctx_blocks, n_qb),
        grid=(bsz, n_qb),
        in_specs=[
            pl.BlockSpec(memory_space=pltpu.SMEM),
            pl.BlockSpec((wb, HQ_B * DH_B), lambda b, qb: (blk(b, qb), OFF_Q_B // (HQ_B * DH_B))),
            pl.BlockSpec((wb, kvw), lambda b, qb: (blk(b, qb), kv_col)),
            pl.BlockSpec((wb, kvw), lambda b, qb: (blk(b, prev(qb)), kv_col)),
            pl.BlockSpec((wb, kvw), lambda b, qb: (blk(b, nxt(qb)), kv_col)),
            pl.BlockSpec((tc, kvw), lambda b, qb: (b * (tt // tc), kv_col)),
            pl.BlockSpec((wb, LANE), lambda b, qb: (qb, 0)),
            pl.BlockSpec((wb, LANE), lambda b, qb: (qb, 0)),
            pl.BlockSpec((wb, LANE), lambda b, qb: (prev(qb), 0)),
            pl.BlockSpec((wb, LANE), lambda b, qb: (prev(qb), 0)),
            pl.BlockSpec((wb, LANE), lambda b, qb: (nxt(qb), 0)),
            pl.BlockSpec((wb, LANE), lambda b, qb: (nxt(qb), 0)),
        ],
        out_specs=pl.BlockSpec((wb, HQ_B * DH_B), lambda b, qb: (blk(b, qb), 0)),
        out_shape=jax.ShapeDtypeStruct((nt, HQ_B * DH_B), F32),
        compiler_params=_cparams(("parallel", "parallel")),
        name="window_attention",
    )(sink, proj, proj, proj, proj, proj, cos_t, sin_t, cos_t, sin_t, cos_t, sin_t)


def rope_tables(tc, tl):
    rows = tl // GRID_W
    row = jnp.repeat(jnp.arange(rows, dtype=F32), GRID_W)
    col = jnp.broadcast_to(jnp.arange(GRID_W, dtype=F32), (rows, GRID_W)).reshape(-1)
    quarter = DH_B // 4
    inv_freq = ROPE_BASE ** (-jnp.arange(quarter, dtype=F32) / quarter)
    ang_r = row[:, None] * inv_freq
    ang_c = col[:, None] * inv_freq
    cr, sr, cc, sc = jnp.cos(ang_r), jnp.sin(ang_r), jnp.cos(ang_c), jnp.sin(ang_c)
    cos_l = jnp.concatenate([cr, cr, cc, cc], -1)
    sin_l = jnp.concatenate([-sr, sr, -sc, sc], -1)
    cos_t = jnp.concatenate([jnp.ones((tc, DH_B), F32), cos_l], 0)
    sin_t = jnp.concatenate([jnp.zeros((tc, DH_B), F32), sin_l], 0)
    reps = LANE // DH_B
    return jnp.tile(cos_t, (1, reps)), jnp.tile(sin_t, (1, reps))


def _layer_norm(x, g, b):
    xc = x - jnp.mean(x, -1, keepdims=True)
    var = jnp.mean(xc * xc, -1, keepdims=True)
    return xc * lax.rsqrt(var + LN_EPS) * g + b


def _head_rms_gate(o, norm_w, gate_raw, dh):
    parts = []
    for h in range(o.shape[-1] // dh):
        seg = o[:, h * dh:(h + 1) * dh]
        parts.append(seg * lax.rsqrt(jnp.mean(seg * seg, -1, keepdims=True) + RMS_EPS) * norm_w)
    return jnp.concatenate(parts, axis=1) * _silu(gate_raw)


def _merge_kernel(alpha, oa_f_ref, oa_b_ref, ga_ref, yb_ref, oc_f_ref, oc_b_ref, gc_ref,
                  gt_a_ref, gt_b_ref, gt_c_ref, x_ref, g1_ref, sc2_ref, sh2_ref, nwa_ref, nwc_ref,
                  wa_ref, wb_ref, wc_ref, wo_ref, lng_ref, lnb_ref, wr_ref, br_ref,
                  x1_ref, h2_ref, lg_ref):
    ya = _head_rms_gate(oa_f_ref[...] + oa_b_ref[...], nwa_ref[...], ga_ref[...], DV_A)
    yc = _head_rms_gate(oc_f_ref[...] + oc_b_ref[...], nwc_ref[...], gc_ref[...], DV_C)
    merged = (_sigmoid(gt_a_ref[...]) * _dot(ya, wa_ref[...])
              + _sigmoid(gt_b_ref[...]) * _dot(yb_ref[...], wb_ref[...])
              + _sigmoid(gt_c_ref[...]) * _dot(yc, wc_ref[...]))
    out = _dot(merged, wo_ref[...])
    x1 = _layer_norm(alpha * x_ref[...] + g1_ref[0] * out, lng_ref[...], lnb_ref[...])
    x1_ref[...] = x1
    h2 = x1 * (1.0 + sc2_ref[0]) + sh2_ref[0]
    h2_ref[...] = h2
    lg_ref[...] = _dot_hi(h2, wr_ref[...]) + br_ref[...]


def branch_merge(alpha, oa_f, oa_b, yb, oc_f, oc_b, proj, x, g1, sc2, sh2, nwa, nwc, wa, wb, wc, wo,
                 ln_g, ln_b, w_route, b_route, sel):
    nt, d = x.shape
    hw = H_A * DV_A
    row = lambda w, col=0: pl.BlockSpec((TM, w), lambda m: (m, col))
    full = lambda shape: pl.BlockSpec(shape, lambda m: (0,) * len(shape))
    mod = pl.BlockSpec((1, 1, d), lambda m: (sel(m), 0, 0))
    return pl.pallas_call(
        functools.partial(_merge_kernel, alpha),
        grid=(nt // TM,),
        in_specs=[row(hw), row(hw), row(hw, OFF_GATE_A // hw), row(hw), row(hw), row(hw), row(hw, OFF_GATE_C // hw),
                  row(d, OFF_GATES // d), row(d, OFF_GATES // d + 1), row(d, OFF_GATES // d + 2),
                  row(d), mod, mod, mod, full((1, DV_A)), full((1, DV_C)),
                  full((hw, d)), full((hw, d)), full((hw, d)), full((d, d)),
                  full((1, d)), full((1, d)), full((d, LANE)), full((1, LANE))],
        out_specs=[row(d), row(d), row(LANE)],
        out_shape=[jax.ShapeDtypeStruct((nt, d), F32), jax.ShapeDtypeStruct((nt, d), F32),
                   jax.ShapeDtypeStruct((nt, LANE), F32)],
        compiler_params=_cparams(("parallel",)),
        name="branch_merge",
    )(oa_f, oa_b, proj, yb, oc_f, oc_b, proj, proj, proj, proj, x, g1, sc2, sh2, nwa, nwc,
      wa, wb, wc, wo, ln_g, ln_b, w_route, b_route)


def _dispatch_kernel(dest_ref, h_ref, xs_in_ref, xs_ref, sem):
    del xs_in_ref
    n = h_ref.shape[0] * TOP_K

    def copy(a):
        return pltpu.make_async_copy(h_ref.at[pl.ds(a // TOP_K, 1), :],
                                     xs_ref.at[pl.ds(dest_ref[0, 0, a], 1), :], sem)

    def start(a, carry):
        copy(a).start()
        return carry

    def wait(a, carry):
        copy(a).wait()
        return carry

    lax.fori_loop(0, n, start, 0)
    lax.fori_loop(0, n, wait, 0)


def moe_dispatch(h2, dest, cap):
    nt, d = h2.shape
    n_m = nt // TM
    xs0 = jnp.zeros((cap, d), F32)
    return pl.pallas_call(
        _dispatch_kernel,
        grid=(n_m,),
        in_specs=[pl.BlockSpec((1, 1, TM * TOP_K), lambda m: (m, 0, 0), memory_space=pltpu.SMEM),
                  pl.BlockSpec((TM, d), lambda m: (m, 0)),
                  pl.BlockSpec(memory_space=pl.ANY)],
        out_specs=pl.BlockSpec(memory_space=pl.ANY),
        out_shape=jax.ShapeDtypeStruct((cap, d), F32),
        scratch_shapes=[pltpu.SemaphoreType.DMA],
        input_output_aliases={2: 0},
        compiler_params=_cparams(("arbitrary",)),
        name="moe_dispatch",
    )(dest.reshape(n_m, 1, TM * TOP_K), h2, xs0)


def _expert_kernel(be_ref, nused_ref, x_ref, w1_ref, w3_ref, w2_ref, y_ref):
    i = pl.program_id(0)

    @pl.when(i < nused_ref[0])
    def _():
        xb = x_ref[...]
        hid = _silu(_dot(xb, w1_ref[0, 0])) * _dot(xb, w3_ref[0, 0])
        y_ref[...] = _dot(hid, w2_ref[0, 0])

    @pl.when(i >= nused_ref[0])
    def _():
        y_ref[...] = jnp.zeros_like(y_ref)


def moe_experts(xs, block_expert, n_used, w1, w3, w2, layer):
    cap, d = xs.shape
    de = w1.shape[-1]
    return pl.pallas_call(
        _expert_kernel,
        grid_spec=pltpu.PrefetchScalarGridSpec(
            num_scalar_prefetch=2,
            grid=(cap // TMB,),
            in_specs=[pl.BlockSpec((TMB, d), lambda i, be, nu: (i, 0)),
                      pl.BlockSpec((1, 1, d, de), lambda i, be, nu: (layer, be[i], 0, 0)),
                      pl.BlockSpec((1, 1, d, de), lambda i, be, nu: (layer, be[i], 0, 0)),
                      pl.BlockSpec((1, 1, de, d), lambda i, be, nu: (layer, be[i], 0, 0))],
            out_specs=pl.BlockSpec((TMB, d), lambda i, be, nu: (i, 0)),
        ),
        out_shape=jax.ShapeDtypeStruct((cap, d), F32),
        compiler_params=_cparams(("arbitrary",)),
        name="moe_experts",
    )(block_expert, n_used, xs, w1, w3, w2)


def _combine_kernel(alpha, dest_ref, y_ref, wt_ref, x1_ref, g2_ref, lng_ref, lnb_ref, o_ref, buf, sem):
    tm = x1_ref.shape[0]
    n = tm * TOP_K

    def copy(a):
        return pltpu.make_async_copy(y_ref.at[pl.ds(dest_ref[0, 0, a], 1), :],
                                     buf.at[a % TOP_K, pl.ds(a // TOP_K, 1), :], sem)

    def start(a, carry):
        copy(a).start()
        return carry

    def wait(a, carry):
        copy(a).wait()
        return carry

    lax.fori_loop(0, n, start, 0)
    lax.fori_loop(0, n, wait, 0)
    wt = wt_ref[...]
    y = wt[:, 0:1] * buf[0] + wt[:, 1:2] * buf[1]
    o_ref[...] = _layer_norm(alpha * x1_ref[...] + g2_ref[0] * y, lng_ref[...], lnb_ref[...])


def moe_combine(alpha, y, dest, wts, x1, g2, ln_g, ln_b, sel):
    nt, d = x1.shape
    n_m = nt // TM
    full = lambda shape: pl.BlockSpec(shape, lambda m: (0,) * len(shape))
    return pl.pallas_call(
        functools.partial(_combine_kernel, alpha),
        grid=(n_m,),
        in_specs=[pl.BlockSpec((1, 1, TM * TOP_K), lambda m: (m, 0, 0), memory_space=pltpu.SMEM),
                  pl.BlockSpec(memory_space=pl.ANY),
                  pl.BlockSpec((TM, TOP_K), lambda m: (m, 0)),
                  pl.BlockSpec((TM, d), lambda m: (m, 0)),
                  pl.BlockSpec((1, 1, d), lambda m: (sel(m), 0, 0)),
                  full((1, d)), full((1, d))],
        out_specs=pl.BlockSpec((TM, d), lambda m: (m, 0)),
        out_shape=jax.ShapeDtypeStruct((nt, d), F32),
        scratch_shapes=[pltpu.VMEM((TOP_K, TM, d), F32), pltpu.SemaphoreType.DMA],
        compiler_params=_cparams(("arbitrary",)),
        name="moe_combine",
    )(dest.reshape(n_m, 1, TM * TOP_K), y, wts, x1, g2, ln_g, ln_b)


def route(logits):
    n_tok = logits.shape[0]
    group_logits = logits[:, :N_GROUPS]
    group_idx = jnp.argmax(group_logits, axis=-1).astype(jnp.int32)
    group_w = jnp.take_along_axis(jax.nn.softmax(group_logits, -1), group_idx[:, None], -1)
    exp_logits = logits[:, N_GROUPS:N_GROUPS + N_EXPERTS].reshape(n_tok, N_GROUPS, EXPERTS_PER_GROUP)
    in_group = jnp.take_along_axis(exp_logits, group_idx[:, None, None], 1)[:, 0]
    top_logit, top_idx = lax.top_k(in_group, TOP_K)
    weights = jax.nn.softmax(top_logit, -1) * group_w
    expert_id = group_idx[:, None] * EXPERTS_PER_GROUP + top_idx.astype(jnp.int32)

    flat_e = expert_id.reshape(-1)
    onehot = (flat_e[:, None] == jnp.arange(N_EXPERTS, dtype=jnp.int32)[None, :]).astype(jnp.int32)
    csum = jnp.cumsum(onehot, axis=0)
    rank = jnp.sum(csum * onehot, axis=1) - 1
    counts = csum[-1]
    padded = (counts + TMB - 1) // TMB * TMB
    pad_end = jnp.cumsum(padded)
    pad_start = pad_end - padded
    dest = (pad_start[flat_e] + rank).astype(jnp.int32)
    n_blocks = -(-(n_tok * TOP_K) // TMB) + N_EXPERTS
    block_expert = jnp.minimum(
        jnp.searchsorted(pad_end, jnp.arange(n_blocks, dtype=jnp.int32) * TMB, side='right'),
        N_EXPERTS - 1).astype(jnp.int32)
    n_used = (pad_end[-1:] // TMB).astype(jnp.int32)
    return dest, weights.astype(F32), block_expert, n_used, n_blocks * TMB


def _permute_w_in(w_in):
    sizes = (A_QKV, 2 * H_A, 2 * H_A, H_A * DV_A, HQ_B * DH_B, HKV_B * DH_B, HKV_B * DH_B,
             H_C * DK_C, 2 * H_C * DK_C, H_C * DV_C, H_C * DV_C, 3 * D_MODEL)
    pts = np.cumsum(sizes)[:-1]
    (qkv_a, beta, alpha, gate_a, q_b, k_b, v_b, q_c, f_c, i_c, gate_c, gates) = jnp.split(w_in, pts, axis=-1)
    lead = w_in.shape[:-1]
    pad_small = jnp.zeros(lead + (LANE - 4 * H_A,), w_in.dtype)
    pad_end = jnp.zeros(lead + (PROJ_W - OFF_SMALL - LANE,), w_in.dtype)
    w_perm = jnp.concatenate([qkv_a, gate_a, q_b, q_c, f_c, i_c, gate_c, gates, k_b, v_b,
                              beta, alpha, pad_small, pad_end], axis=-1)
    w_small_t = jnp.swapaxes(jnp.concatenate([beta, alpha], axis=-1), -1, -2)
    return w_perm.astype(BF16), w_small_t.astype(BF16)


def kernel(x, c, ctx, c_ctx, w_mod, b_mod, w_in, conv_a, gdn_a_log, gdn_dt_bias, gdn_norm_w, attn_sink,
           hgrn_lb_logits, hgrn_norm_w, w_branch_a, w_branch_b, w_branch_c, w_out, ln1_g, ln1_b, ln2_g,
           ln2_b, w_group, b_group, w_router, b_router, w1, w3, w2):
    bsz, tl, d = x.shape
    tc = ctx.shape[1]
    depth = w_mod.shape[0]
    tt = tc + tl
    nt = bsz * tt
    ncc, nch = tc // CHUNK, tt // CHUNK
    alpha = (2 * depth) ** 0.25
    sel = _row_sel(bsz, tt // TM, tc // TM)

    lb_soft = jax.nn.softmax(hgrn_lb_logits.astype(F32), axis=1)
    lower_bounds = jnp.cumsum(lb_soft, axis=1) - lb_soft[:, :1]
    cos_t, sin_t = rope_tables(tc, tl)

    c_rows = jnp.zeros((SUBLANE, d), F32).at[:bsz].set(c).at[bsz].set(c_ctx)
    mods = modulation(c_rows, w_mod, b_mod)
    mods = mods.reshape(depth, SUBLANE, 6, 1, d).transpose(0, 2, 1, 3, 4)

    w_perm, w_small_t = _permute_w_in(w_in)
    wa, wb, wc, wo = (w.astype(BF16) for w in (w_branch_a, w_branch_b, w_branch_c, w_out))
    w1b, w3b, w2b = w1.astype(BF16), w3.astype(BF16), w2.astype(BF16)
    pad_r = jnp.zeros((depth, d, LANE - N_GROUPS - N_EXPERTS), F32)
    w_route = jnp.concatenate([w_group, w_router, pad_r], axis=-1)
    b_route = jnp.concatenate([b_group, b_router, jnp.zeros((depth, LANE - N_GROUPS - N_EXPERTS), F32)], -1)

    xs = jnp.concatenate([ctx, x], axis=1).reshape(nt, d)
    for l in range(depth):
        sh1, sc1, g1, sh2, sc2, g2 = (mods[l, i] for i in range(6))
        proj, small_t = in_projection(xs, sc1, sh1, w_perm[l], w_small_t[l], sel)
        small_rows = small_t.reshape(2 * SUBLANE, nt // CHUNK, CHUNK).transpose(1, 0, 2)
        oa_f = gdn_direction(proj, small_rows, conv_a[l], gdn_a_log[l], gdn_dt_bias[l], bsz, ncc, nch, False)
        oa_b = gdn_direction(proj, small_rows, conv_a[l], gdn_a_log[l], gdn_dt_bias[l], bsz, ncc, nch, True)
        yb = window_attention(proj, attn_sink[l], cos_t, sin_t, bsz, tt, tc)
        oc_f = hgrn_direction(proj, lower_bounds[:, l], bsz, ncc, nch, False)
        oc_b = hgrn_direction(proj, lower_bounds[:, l], bsz, ncc, nch, True)
        x1, h2, logits = branch_merge(
            alpha, oa_f, oa_b, yb, oc_f, oc_b, proj, xs, g1, sc2, sh2,
            gdn_norm_w[l].reshape(1, -1), hgrn_norm_w[l].reshape(1, -1), wa[l], wb[l], wc[l], wo[l],
            ln1_g[l].reshape(1, d), ln1_b[l].reshape(1, d), w_route[l], b_route[l].reshape(1, LANE), sel)
        dest, wts, block_expert, n_used, cap = route(logits)
        slots = moe_dispatch(h2, dest, cap)
        y = moe_experts(slots, block_expert, n_used, w1b, w3b, w2b, l)
        xs = moe_combine(alpha, y, dest, wts, x1, g2, ln2_g[l].reshape(1, d), ln2_b[l].reshape(1, d), sel)
    return xs.reshape(bsz, tt, d)[:, tc:]
```
